```python
import math
import jax, jax.numpy as jnp
from jax import lax
import numpy as np

D_MODEL = 1024
BATCH = 16
SEQ = 2048
DEPTH = 2

N_BRANCH = 4
HEAD_DIM = 64
BR_HEADS = 4
BR_WIDTH = BR_HEADS * HEAD_DIM
Q_BLOCK = 128
SB_HEADS = BR_HEADS
RW_HEADS = BR_HEADS
RW_N = HEAD_DIM
RW_W_LORA = 64
RW_A_LORA = 64
RW_G_LORA = 128
RW_GN_EPS = 64e-5
RW_IN_SIZES = (BR_WIDTH, BR_WIDTH, BR_WIDTH, RW_W_LORA, RW_A_LORA, RW_G_LORA)
SSD_HEADS = BR_HEADS
SSD_P = HEAD_DIM
SSD_GROUPS = 2
SSD_N = 128
SSD_CONV = 4
SSD_CHUNK = 128
SSD_XBC = BR_WIDTH + 2 * SSD_GROUPS * SSD_N
SSD_IN_SIZES = (BR_WIDTH, SSD_XBC, SSD_HEADS)
MLA_HEADS = BR_HEADS
MLA_Q_RANK = 256
MLA_KV_RANK = 128
MLA_NOPE = 64
MLA_ROPE = 32
MLA_V = 64
ROPE_BASE = 10000.0
MLA_IN_SIZES = (MLA_Q_RANK, MLA_KV_RANK, MLA_ROPE)
MIX_IN_SIZES = (3 * BR_WIDTH, sum(RW_IN_SIZES), sum(SSD_IN_SIZES), sum(MLA_IN_SIZES))
N_IN = sum(MIX_IN_SIZES)
MEM_TOKENS = 256
XA_HEADS = 4
XA_DH = D_MODEL // XA_HEADS
FFN_DENSE = 2816
N_EXPERTS = 8
TOP_K = 2
FFN_EXPERT = 3584
MOE_BLOCK = 256
N_DENSE_LAYERS = (DEPTH + 1) // 2
N_MOE_LAYERS = DEPTH // 2
DN_ALPHA = (2.0 * DEPTH) ** 0.25
DN_BETA = (8.0 * DEPTH) ** -0.25

kernel_name = "hybrid_sb_rwkv7_ssd_mla_moe_deepnorm"


def _offsets(sizes):
    return np.cumsum(sizes)[:-1].tolist()


def _layer_norm(x, g, b, eps=1e-5):
    xf = x.astype(jnp.float32)
    mu = jnp.mean(xf, axis=-1, keepdims=True)
    var = jnp.mean(jnp.square(xf - mu), axis=-1, keepdims=True)
    return ((xf - mu) * lax.rsqrt(var + eps) * g + b).astype(x.dtype)


def _rms_norm(x, g, eps=1e-6):
    xf = x.astype(jnp.float32)
    return (xf * lax.rsqrt(jnp.mean(xf * xf, axis=-1, keepdims=True) + eps) * g).astype(x.dtype)


def _rope(x, cos, sin):
    x1, x2 = jnp.split(x, 2, axis=-1)
    return jnp.concatenate([x1 * cos - x2 * sin, x1 * sin + x2 * cos], axis=-1)


def _stick_breaking_attention(q, k, v):
    s_len = q.shape[1]
    scale = HEAD_DIM ** -0.5
    outs = []
    for blk in range(s_len // Q_BLOCK):
        lo, hi = blk * Q_BLOCK, (blk + 1) * Q_BLOCK
        z = jnp.einsum("bqhd,bkhd->bhqk", q[:, lo:hi], k[:, :hi]).astype(jnp.float32) * scale
        t_pos = lo + jnp.arange(Q_BLOCK)
        s_pos = jnp.arange(hi)
        before = s_pos[None, :] < t_pos[:, None]
        log_keep = jnp.where(before, jax.nn.log_sigmoid(-z), 0.0)
        tail = lax.cumsum(log_keep, axis=3, reverse=True) - log_keep
        w = jnp.where(before, jnp.exp(jax.nn.log_sigmoid(z) + tail), 0.0)
        outs.append(jnp.einsum("bhqk,bkhd->bqhd", w.astype(v.dtype), v[:, :hi]))
    return jnp.concatenate(outs, axis=1)


def _rwkv7_scan(r, decay, k, v, kk, a):
    def step(state, inp):
        r_t, w_t, k_t, v_t, kk_t, a_t = inp
        sa = jnp.einsum("bhij,bhj->bhi", state, kk_t)
        state = (state * w_t[:, :, None, :]
                 - sa[..., None] * (kk_t * a_t)[:, :, None, :]
                 + v_t[..., None] * k_t[:, :, None, :])
        return state, jnp.einsum("bhij,bhj->bhi", state, r_t)

    xs = tuple(jnp.moveaxis(t, 1, 0) for t in (r, decay, k, v, kk, a))
    state0 = jnp.zeros((r.shape[0], RW_HEADS, RW_N, RW_N), jnp.float32)
    _, ys = lax.scan(step, state0, xs)
    return jnp.moveaxis(ys, 0, 1)


def _rwkv7_time_mix(cols, mu, w0, w_up, a0, a_up, g_up, k_k, k_a, r_k, gn_g, gn_b):
    b, s, _ = cols.shape
    p = cols.astype(jnp.float32)
    prev = jnp.pad(p, ((0, 0), (1, 0), (0, 0)))[:, :-1]
    p = p + (prev - p) * mu
    r, k, v, w_lo, a_lo, g_lo = jnp.split(p, _offsets(RW_IN_SIZES), axis=-1)
    w_log = -jax.nn.softplus(-(w0 + jnp.tanh(w_lo) @ w_up)) - 0.5
    decay = jnp.exp(-jnp.exp(w_log))
    a = jax.nn.sigmoid(a0 + a_lo @ a_up)
    g = jax.nn.sigmoid(g_lo) @ g_up
    kk = k * k_k
    k = k * (1.0 + (a - 1.0) * k_a)

    def heads(t):
        return t.reshape(b, s, RW_HEADS, RW_N)

    r, k, v, decay, a, kk = heads(r), heads(k), heads(v), heads(decay), heads(a), heads(kk)
    kk = kk * lax.rsqrt(jnp.maximum(jnp.sum(kk * kk, axis=-1, keepdims=True), 1e-24))
    y = _rwkv7_scan(r, decay, k, v, kk, a)
    y_mu = jnp.mean(y, axis=-1, keepdims=True)
    y_var = jnp.mean(jnp.square(y - y_mu), axis=-1, keepdims=True)
    y = ((y - y_mu) * lax.rsqrt(y_var + RW_GN_EPS)).reshape(b, s, BR_WIDTH) * gn_g + gn_b
    bonus = jnp.sum(r * k * r_k, axis=-1, keepdims=True) * v
    return ((y + bonus.reshape(b, s, BR_WIDTH)) * g).astype(cols.dtype)


def _segsum(v):
    t = v.shape[-1]
    rep = jnp.broadcast_to(v[..., :, None], v.shape + (t,))
    strict = jnp.tril(jnp.ones((t, t), dtype=bool), -1)
    cs = jnp.cumsum(jnp.where(strict, rep, 0.0), axis=-2)
    return jnp.where(jnp.tril(jnp.ones((t, t), dtype=bool)), cs, -jnp.inf)


def _ssd_chunked(xh, dt, a, bh, ch):
    b, s, h, p = xh.shape
    n = bh.shape[-1]
    c = s // SSD_CHUNK
    xd = (xh * dt[..., None]).reshape(b, c, SSD_CHUNK, h, p)
    bc = bh.reshape(b, c, SSD_CHUNK, h, n)
    cc = ch.reshape(b, c, SSD_CHUNK, h, n)
    a_dt = jnp.transpose((dt * a).reshape(b, c, SSD_CHUNK, h), (0, 3, 1, 2))
    a_cs = jnp.cumsum(a_dt, axis=-1)
    decay_in = jnp.exp(_segsum(a_dt))
    cb = jnp.einsum("bclhn,bcshn->bhcls", cc, bc)
    y_diag = jnp.einsum("bhcls,bcshp->bclhp", cb * decay_in, xd)
    decay_to_end = jnp.exp(a_cs[..., -1:] - a_cs)
    states = jnp.einsum("bclhn,bhcl,bclhp->bchpn", bc, decay_to_end, xd)
    states = jnp.concatenate([jnp.zeros_like(states[:, :1]), states], axis=1)
    chunk_a = jnp.pad(a_cs[..., -1], ((0, 0), (0, 0), (1, 0)))
    decay_chunk = jnp.exp(_segsum(chunk_a))
    states = jnp.einsum("bhzc,bchpn->bzhpn", decay_chunk, states)[:, :-1]
    y_off = jnp.einsum("bclhn,bchpn,bhcl->bclhp", cc, states, jnp.exp(a_cs))
    return (y_diag + y_off).reshape(b, s, h, p)


def _causal_depthwise_conv(x, w, bias):
    ch = x.shape[-1]
    y = lax.conv_general_dilated(x, w[:, None, :].astype(x.dtype), window_strides=(1,),
                                 padding=[(SSD_CONV - 1, 0)],
                                 dimension_numbers=("NWC", "WIO", "NWC"),
                                 feature_group_count=ch)
    return y + bias


def _mamba2_ssd_mix(z, xbc, dt_raw, conv_w, conv_b, dt_bias, a_log, d_skip, norm_g):
    b, s, _ = z.shape
    xbc = jax.nn.silu(_causal_depthwise_conv(xbc, conv_w, conv_b)).astype(jnp.float32)
    xs, bm, cm = jnp.split(xbc, _offsets((BR_WIDTH, SSD_GROUPS * SSD_N, SSD_GROUPS * SSD_N)), axis=-1)
    xh = xs.reshape(b, s, SSD_HEADS, SSD_P)
    rep = SSD_HEADS // SSD_GROUPS
    bh = jnp.repeat(bm.reshape(b, s, SSD_GROUPS, SSD_N), rep, axis=2)
    ch = jnp.repeat(cm.reshape(b, s, SSD_GROUPS, SSD_N), rep, axis=2)
    dt = jax.nn.softplus(dt_raw.astype(jnp.float32) + dt_bias)
    a = -jnp.exp(a_log.astype(jnp.float32))
    y = _ssd_chunked(xh, dt, a, bh, ch) + d_skip[:, None] * xh
    y = y.reshape(b, s, BR_WIDTH) * jax.nn.silu(z.astype(jnp.float32))
    yg = y.reshape(b, s, SSD_GROUPS, BR_WIDTH // SSD_GROUPS)
    yg = yg * lax.rsqrt(jnp.mean(yg * yg, axis=-1, keepdims=True) + 1e-5)
    return (yg.reshape(b, s, BR_WIDTH) * norm_g).astype(z.dtype)


def _mla_attention(q_nope, q_rope, k_nope, k_rope, v):
    s_len = q_nope.shape[1]
    scale = (MLA_NOPE + MLA_ROPE) ** -0.5
    outs = []
    for blk in range(s_len // Q_BLOCK):
        lo, hi = blk * Q_BLOCK, (blk + 1) * Q_BLOCK
        z = (jnp.einsum("bqhd,bkhd->bhqk", q_nope[:, lo:hi], k_nope[:, :hi])
             + jnp.einsum("bqhr,bkr->bhqk", q_rope[:, lo:hi], k_rope[:, :hi])).astype(jnp.float32) * scale
        causal = jnp.arange(hi)[None, :] <= (lo + jnp.arange(Q_BLOCK))[:, None]
        prob = jax.nn.softmax(jnp.where(causal, z, -jnp.inf), axis=-1)
        outs.append(jnp.einsum("bhqk,bkhd->bqhd", prob.astype(v.dtype), v[:, :hi]))
    return jnp.concatenate(outs, axis=1)


def _mla_mix(cq, ckv, kr_raw, cos, sin, q_norm_g, w_uq, kv_norm_g, w_ukv):
    b, s, _ = cq.shape
    q = (_rms_norm(cq, q_norm_g) @ w_uq).reshape(b, s, MLA_HEADS, MLA_NOPE + MLA_ROPE)
    q_nope, q_rope = q[..., :MLA_NOPE], _rope(q[..., MLA_NOPE:], cos[:, :, None, :], sin[:, :, None, :])
    kv = (_rms_norm(ckv, kv_norm_g) @ w_ukv).reshape(b, s, MLA_HEADS, MLA_NOPE + MLA_V)
    k_nope, v = kv[..., :MLA_NOPE], kv[..., MLA_NOPE:]
    k_rope = _rope(kr_raw, cos, sin)
    return _mla_attention(q_nope, q_rope, k_nope, k_rope, v).reshape(b, s, BR_WIDTH)


def _hybrid_mixer(x, cos, sin, w_in, rw_mu, rw_w0, rw_w_up, rw_a0, rw_a_up, rw_g_up, rw_k_k, rw_k_a,
                  rw_r_k, rw_gn_g, rw_gn_b, ssd_conv_w, ssd_conv_b, ssd_dt_bias, ssd_a_log, ssd_d,
                  ssd_norm_g, mla_q_norm_g, mla_w_uq, mla_kv_norm_g, mla_w_ukv, w_br_out, w_gate, w_out):
    b, s, _ = x.shape
    sb_cols, rw_cols, ssd_cols, mla_cols = jnp.split(x @ w_in, _offsets(MIX_IN_SIZES), axis=-1)
    q, k, v = (t.reshape(b, s, SB_HEADS, HEAD_DIM) for t in jnp.split(sb_cols, 3, axis=-1))
    y_sb = _stick_breaking_attention(q, k, v).reshape(b, s, BR_WIDTH)
    y_rw = _rwkv7_time_mix(rw_cols, rw_mu, rw_w0, rw_w_up, rw_a0, rw_a_up, rw_g_up, rw_k_k, rw_k_a,
                           rw_r_k, rw_gn_g, rw_gn_b)
    z, xbc, dt_raw = jnp.split(ssd_cols, _offsets(SSD_IN_SIZES), axis=-1)
    y_ssd = _mamba2_ssd_mix(z, xbc, dt_raw, ssd_conv_w, ssd_conv_b, ssd_dt_bias, ssd_a_log, ssd_d, ssd_norm_g)
    cq, ckv, kr_raw = jnp.split(mla_cols, _offsets(MLA_IN_SIZES), axis=-1)
    y_mla = _mla_mix(cq, ckv, kr_raw, cos, sin, mla_q_norm_g, mla_w_uq, mla_kv_norm_g, mla_w_ukv)
    merged = jnp.zeros_like(x)
    for i, y_br in enumerate((y_sb, y_rw, y_ssd, y_mla)):
        merged = merged + jax.nn.sigmoid(x @ w_gate[i]) * (y_br @ w_br_out[i])
    return merged @ w_out


def _memory_cross_attention(x, mem, w_q, w_kv, w_o):
    b, s, d = x.shape
    m = mem.shape[1]
    q = (x @ w_q).reshape(b, s, XA_HEADS, XA_DH)
    k, v = jnp.split(mem @ w_kv, 2, axis=-1)
    k = k.reshape(b, m, XA_HEADS, XA_DH)
    v = v.reshape(b, m, XA_HEADS, XA_DH)
    z = jnp.einsum("bshd,bmhd->bhsm", q, k).astype(jnp.float32) * (XA_DH ** -0.5)
    prob = jax.nn.softmax(z, axis=-1)
    o = jnp.einsum("bhsm,bmhd->bshd", prob.astype(v.dtype), v).reshape(b, s, d)
    return o @ w_o


def _swiglu(x, w13, w2):
    gate, up = jnp.split(x @ w13, 2, axis=-1)
    return (jax.nn.silu(gate) * up) @ w2


def _moe_swiglu(xf, router, w13, w2):
    t, d = xf.shape
    logits = (xf @ router).astype(jnp.float32)
    top_v, top_i = lax.top_k(logits, TOP_K)
    gates = jax.nn.softmax(top_v, axis=-1)
    n_assign = t * TOP_K
    e_flat = top_i.reshape(-1)
    tok_flat = jnp.repeat(jnp.arange(t, dtype=jnp.int32), TOP_K)
    w_flat = gates.reshape(-1).astype(xf.dtype)
    order = jnp.argsort(e_flat)
    e_s, tok_s, w_s = e_flat[order], tok_flat[order], w_flat[order]
    counts = jnp.bincount(e_flat, length=N_EXPERTS)
    starts = jnp.cumsum(counts) - counts
    padded = (counts + MOE_BLOCK - 1) // MOE_BLOCK * MOE_BLOCK
    p_ends = jnp.cumsum(padded)
    p_starts = p_ends - padded
    dest = p_starts[e_s] + (jnp.arange(n_assign) - starts[e_s])
    n_blocks = -(-n_assign // MOE_BLOCK) + N_EXPERTS
    n_rows = n_blocks * MOE_BLOCK
    row_tok = jnp.zeros((n_rows,), jnp.int32).at[dest].set(tok_s)
    row_w = jnp.zeros((n_rows,), xf.dtype).at[dest].set(w_s)
    block_e = jnp.minimum(jnp.searchsorted(p_ends, jnp.arange(n_blocks) * MOE_BLOCK, side="right"),
                          N_EXPERTS - 1)
    xs = xf[row_tok].reshape(n_blocks, MOE_BLOCK, d)

    def expert_block(args):
        xb, e = args
        gate, up = jnp.split(xb @ w13[e], 2, axis=-1)
        return (jax.nn.silu(gate) * up) @ w2[e]

    ys = lax.map(expert_block, (xs, block_e)).reshape(n_rows, d)
    return jnp.zeros_like(xf).at[row_tok].add(ys * row_w[:, None])


def setup_inputs(seed: int = 0) -> dict:
    key = jax.random.key(seed)
    ks = iter(jax.random.split(key, 64))
    L = DEPTH
    f32 = jnp.float32

    def nrm(shape, scale):
        return jax.random.normal(next(ks), shape, f32) * scale

    def gain(shape):
        return 1.0 + nrm(shape, 0.02)

    def unif(shape, lo, hi):
        return jax.random.uniform(next(ks), shape, f32, lo, hi)

    dt0 = jnp.exp(unif((L, SSD_HEADS), math.log(1e-3), math.log(1e-1)))
    return {
        "x": nrm((BATCH, SEQ, D_MODEL), 1.0),
        "mem": nrm((BATCH, MEM_TOKENS, D_MODEL), 1.0),
        "positions": jnp.arange(SEQ, dtype=jnp.int32)[None, :]
                     + jax.random.randint(next(ks), (BATCH, 1), 0, 4096, dtype=jnp.int32),
        "mix_w_in": nrm((L, D_MODEL, N_IN), D_MODEL ** -0.5),
        "rw_mu": unif((L, sum(RW_IN_SIZES)), 0.0, 1.0),
        "rw_w0": unif((L, BR_WIDTH), -6.0, 1.0),
        "rw_w_up": nrm((L, RW_W_LORA, BR_WIDTH), 0.5 * RW_W_LORA ** -0.5),
        "rw_a0": nrm((L, BR_WIDTH), 0.1),
        "rw_a_up": nrm((L, RW_A_LORA, BR_WIDTH), RW_A_LORA ** -0.5),
        "rw_g_up": nrm((L, RW_G_LORA, BR_WIDTH), RW_G_LORA ** -0.5),
        "rw_k_k": 0.85 + nrm((L, BR_WIDTH), 0.05),
        "rw_k_a": 1.0 + nrm((L, BR_WIDTH), 0.05),
        "rw_r_k": nrm((L, RW_HEADS, RW_N), 0.1),
        "rw_gn_g": gain((L, BR_WIDTH)),
        "rw_gn_b": nrm((L, BR_WIDTH), 0.02),
        "ssd_conv_w": nrm((L, SSD_CONV, SSD_XBC), SSD_CONV ** -0.5),
        "ssd_conv_b": nrm((L, SSD_XBC), 0.02),
        "ssd_dt_bias": dt0 + jnp.log(-jnp.expm1(-dt0)),
        "ssd_a_log": jnp.log(unif((L, SSD_HEADS), 1.0, 16.0)),
        "ssd_d": 1.0 + nrm((L, SSD_HEADS), 0.1),
        "ssd_norm_g": gain((L, BR_WIDTH)),
        "mla_q_norm_g": gain((L, MLA_Q_RANK)),
        "mla_w_uq": nrm((L, MLA_Q_RANK, MLA_HEADS * (MLA_NOPE + MLA_ROPE)), MLA_Q_RANK ** -0.5),
        "mla_kv_norm_g": gain((L, MLA_KV_RANK)),
        "mla_w_ukv": nrm((L, MLA_KV_RANK, MLA_HEADS * (MLA_NOPE + MLA_V)), MLA_KV_RANK ** -0.5),
        "mix_w_br_out": nrm((L, N_BRANCH, BR_WIDTH, D_MODEL), BR_WIDTH ** -0.5),
        "mix_w_gate": nrm((L, N_BRANCH, D_MODEL, D_MODEL), D_MODEL ** -0.5),
        "mix_w_out": nrm((L, D_MODEL, D_MODEL), DN_BETA * D_MODEL ** -0.5),
        "ln1_g": gain((L, D_MODEL)),
        "ln1_b": nrm((L, D_MODEL), 0.02),
        "xa_w_q": nrm((L, D_MODEL, D_MODEL), D_MODEL ** -0.5),
        "xa_w_kv": nrm((L, D_MODEL, 2 * D_MODEL), D_MODEL ** -0.5),
        "xa_w_o": nrm((L, D_MODEL, D_MODEL), DN_BETA * D_MODEL ** -0.5),
        "ln2_g": gain((L, D_MODEL)),
        "ln2_b": nrm((L, D_MODEL), 0.02),
        "ffn_w13": nrm((N_DENSE_LAYERS, D_MODEL, 2 * FFN_DENSE), D_MODEL ** -0.5),
        "ffn_w2": nrm((N_DENSE_LAYERS, FFN_DENSE, D_MODEL), DN_BETA * FFN_DENSE ** -0.5),
        "moe_router": nrm((N_MOE_LAYERS, D_MODEL, N_EXPERTS), D_MODEL ** -0.5),
        "moe_w13": nrm((N_MOE_LAYERS, N_EXPERTS, D_MODEL, 2 * FFN_EXPERT), D_MODEL ** -0.5),
        "moe_w2": nrm((N_MOE_LAYERS, N_EXPERTS, FFN_EXPERT, D_MODEL), DN_BETA * FFN_EXPERT ** -0.5),
        "ln3_g": gain((L, D_MODEL)),
        "ln3_b": nrm((L, D_MODEL), 0.02),
    }


def reference(x, mem, positions, mix_w_in, rw_mu, rw_w0, rw_w_up, rw_a0, rw_a_up, rw_g_up, rw_k_k, rw_k_a,
              rw_r_k, rw_gn_g, rw_gn_b, ssd_conv_w, ssd_conv_b, ssd_dt_bias, ssd_a_log, ssd_d, ssd_norm_g,
              mla_q_norm_g, mla_w_uq, mla_kv_norm_g, mla_w_ukv, mix_w_br_out, mix_w_gate, mix_w_out,
              ln1_g, ln1_b, xa_w_q, xa_w_kv, xa_w_o, ln2_g, ln2_b, ffn_w13, ffn_w2, moe_router, moe_w13,
              moe_w2, ln3_g, ln3_b):
    inv_freq = ROPE_BASE ** (-jnp.arange(0, MLA_ROPE, 2, dtype=jnp.float32) / MLA_ROPE)
    ang = positions.astype(jnp.float32)[..., None] * inv_freq
    cos, sin = jnp.cos(ang).astype(x.dtype), jnp.sin(ang).astype(x.dtype)
    for l in range(DEPTH):
        mix = _hybrid_mixer(x, cos, sin, mix_w_in[l], rw_mu[l], rw_w0[l], rw_w_up[l], rw_a0[l], rw_a_up[l],
                            rw_g_up[l], rw_k_k[l], rw_k_a[l], rw_r_k[l], rw_gn_g[l], rw_gn_b[l],
                            ssd_conv_w[l], ssd_conv_b[l], ssd_dt_bias[l], ssd_a_log[l], ssd_d[l],
                            ssd_norm_g[l], mla_q_norm_g[l], mla_w_uq[l], mla_kv_norm_g[l], mla_w_ukv[l],
                            mix_w_br_out[l], mix_w_gate[l], mix_w_out[l])
        x = _layer_norm(DN_ALPHA * x + mix, ln1_g[l], ln1_b[l])
        xa = _memory_cross_attention(x, mem, xa_w_q[l], xa_w_kv[l], xa_w_o[l])
        x = _layer_norm(DN_ALPHA * x + xa, ln2_g[l], ln2_b[l])
        if l % 2 == 0:
            ffn = _swiglu(x, ffn_w13[l // 2], ffn_w2[l // 2])
        else:
            ffn = _moe_swiglu(x.reshape(-1, D_MODEL), moe_router[l // 2], moe_w13[l // 2],
                              moe_w2[l // 2]).reshape(x.shape)
        x = _layer_norm(DN_ALPHA * x + ffn, ln3_g[l], ln3_b[l])
    return x
```

```python
import functools
import math

import numpy as np
import jax
import jax.numpy as jnp
from jax import lax
from jax.experimental import pallas as pl
from jax.experimental.pallas import tpu as pltpu

F32 = jnp.float32
BF16 = jnp.bfloat16
HI = lax.Precision.HIGHEST
NT = (((1,), (1,)), ((), ()))
TN = (((0,), (0,)), ((), ()))

D = 1024
DEPTH = 2
HEAD = 64
BRW = 256
RW_CHUNK = 64
SSD_CHUNK = 128
SSD_N = 128
MLA_Q_RANK, MLA_KV_RANK, MLA_ROPE = 256, 128, 32
MLA_HEADS = 4
MLA_DQK = 96
XA_HEADS, XA_DH = 4, 256
MEM = 256
FFN_DENSE = 2816
FFN_EXPERT = 3584
N_EXPERTS = 8
RW_GN_EPS = 64e-5
DN_ALPHA = (2.0 * DEPTH) ** 0.25
ROPE_BASE = 10000.0

VMEM_LIMIT_BYTES = 56 * 1024 * 1024
ROW_TILE = 512
ATT_BLOCK = 128
MOE_ROWS = 512
MOE_FCHUNK = 512
FFN_FCHUNK = 256


def _cp(*sem):
    return pltpu.CompilerParams(dimension_semantics=sem, vmem_limit_bytes=VMEM_LIMIT_BYTES)


def _sigmoid(x):
    return 1.0 / (1.0 + jnp.exp(-x))


def _softplus(x):
    return jnp.maximum(x, 0.0) + jnp.log(1.0 + jnp.exp(-jnp.abs(x)))


def _layer_norm(h, g, b):
    mu = jnp.mean(h, axis=-1, keepdims=True)
    d = h - mu
    var = jnp.mean(d * d, axis=-1, keepdims=True)
    return d * lax.rsqrt(var + 1e-5) * g + b


def _dot(a, b, precision=None):
    return jnp.dot(a, b, preferred_element_type=F32, precision=precision)


def _dg(a, b, dims, precision=None):
    return lax.dot_general(a, b, dims, preferred_element_type=F32, precision=precision)


def _full(shape):
    nd = len(shape)
    return pl.BlockSpec(shape, lambda *_: (0,) * nd)


def _rowmm_kernel(x_ref, w_ref, *o_refs, widths):
    xb = x_ref[...].astype(BF16)
    off = 0
    for o_ref, n in zip(o_refs, widths):
        o_ref[...] = _dot(xb, w_ref[:, off:off + n]).astype(o_ref.dtype)
        off += n


def _rowmm(x, w, outs, tm):
    t, k = x.shape
    widths = tuple(n for n, _ in outs)
    return pl.pallas_call(
        functools.partial(_rowmm_kernel, widths=widths),
        grid=(t // tm,),
        in_specs=[pl.BlockSpec((tm, k), lambda i: (i, 0)), _full((k, sum(widths)))],
        out_specs=[pl.BlockSpec((tm, n), lambda i: (i, 0)) for n in widths],
        out_shape=[jax.ShapeDtypeStruct((t, n), dt) for n, dt in outs],
        compiler_params=_cp("parallel"),
        name="rowmm",
    )(x, w)


def _sb_kernel(q_ref, k_ref, v_ref, o_ref, *, blk):
    i = pl.program_id(2)
    q = q_ref[0]
    first = lax.broadcasted_iota(jnp.int32, (blk, 128), 1) < HEAD
    zero = jnp.zeros_like(q)
    qs = (jnp.where(first, q, zero), jnp.where(first, zero, q))
    row = lax.broadcasted_iota(jnp.int32, (blk, blk), 0)
    col = lax.broadcasted_iota(jnp.int32, (blk, blk), 1)
    before = col < row
    later = (row > col).astype(BF16)

    def step(j, carry, masked):
        c0, c1, acc = carry
        start = pl.multiple_of(j * blk, blk)
        k = k_ref[0, pl.ds(start, blk), :]
        v = v_ref[0, pl.ds(start, blk), :]
        outs, cs = [], []
        for qh, c in zip(qs, (c0, c1)):
            z = _dg(qh, k, NT)
            ls = jnp.minimum(z, 0.0) - jnp.log(1.0 + jnp.exp(-jnp.abs(z)))
            lk = ls - z
            if masked:
                lk = jnp.where(before, lk, 0.0)
            lk_hi = lk.astype(BF16)
            lk_lo = (lk - lk_hi.astype(F32)).astype(BF16)
            tail = _dot(lk_hi, later) + _dot(lk_lo, later)
            w = jnp.exp(ls + tail + c)
            if masked:
                w = jnp.where(before, w, 0.0)
            cs.append(c + jnp.sum(lk, axis=-1, keepdims=True))
            outs.append(_dot(w.astype(BF16), v))
        return cs[0], cs[1], acc + jnp.where(first, outs[0], outs[1])

    zc = jnp.zeros((blk, 1), F32)
    carry = step(i, (zc, zc, jnp.zeros((blk, 128), F32)), True)
    carry = lax.fori_loop(0, i, lambda jj, cr: step(i - 1 - jj, cr, False), carry)
    o_ref[0] = carry[2].astype(o_ref.dtype)


def _sb_attention(q, k, v):
    b, s, _ = q.shape
    blk = ATT_BLOCK
    return pl.pallas_call(
        functools.partial(_sb_kernel, blk=blk),
        grid=(b, 2, s // blk),
        in_specs=[pl.BlockSpec((1, blk, 128), lambda bi, p, i: (bi, i, p)),
                  pl.BlockSpec((1, s, 128), lambda bi, p, i: (bi, 0, p)),
                  pl.BlockSpec((1, s, 128), lambda bi, p, i: (bi, 0, p))],
        out_specs=pl.BlockSpec((1, blk, 128), lambda bi, p, i: (bi, i, p)),
        out_shape=jax.ShapeDtypeStruct((b, s, BRW), BF16),
        compiler_params=_cp("parallel", "parallel", "arbitrary"),
        name="sb_attention",
    )(q, k, v)


def _mla_prep_kernel(c_ref, cos_ref, sin_ref, gq_ref, gkv_ref, wq_ref, wqr_ref, wk_ref, wv_ref,
                     q_ref, k_ref, v_ref):
    c = c_ref[...]
    cq, ckv = c[:, :256], c[:, 256:384]
    kr, kr_rot = c[:, 384:512], c[:, 512:640]
    cqn = (cq * lax.rsqrt(jnp.mean(cq * cq, axis=-1, keepdims=True) + 1e-6) * gq_ref[...]).astype(BF16)
    ckvn = (ckv * lax.rsqrt(jnp.mean(ckv * ckv, axis=-1, keepdims=True) + 1e-6) * gkv_ref[...]).astype(BF16)
    cos1, sin1 = cos_ref[...], sin_ref[...]
    cos4 = jnp.concatenate([cos1] * MLA_HEADS, axis=1)
    sin4 = jnp.concatenate([sin1] * MLA_HEADS, axis=1)
    q = _dot(cqn, wq_ref[...]) * cos4 + _dot(cqn, wqr_ref[...]) * sin4
    q_ref[...] = (q * (MLA_DQK ** -0.5)).astype(BF16)
    k_rope = kr * cos1 + kr_rot * sin1
    k = _dot(ckvn, wk_ref[...]) + jnp.concatenate([k_rope] * MLA_HEADS, axis=1)
    k_ref[...] = k.astype(BF16)
    v_ref[...] = _dot(ckvn, wv_ref[...]).astype(BF16)


def _mla_prep(c, cos_t, sin_t, gq, gkv, wq, wqr, wk, wv, tm):
    t = c.shape[0]
    row = lambda n: pl.BlockSpec((tm, n), lambda i: (i, 0))
    return pl.pallas_call(
        _mla_prep_kernel,
        grid=(t // tm,),
        in_specs=[row(640), row(128), row(128), _full((1, 256)), _full((1, 128)),
                  _full((256, 512)), _full((256, 512)), _full((128, 512)), _full((128, 256))],
        out_specs=[row(512), row(512), row(256)],
        out_shape=[jax.ShapeDtypeStruct((t, 512), BF16), jax.ShapeDtypeStruct((t, 512), BF16),
                   jax.ShapeDtypeStruct((t, 256), BF16)],
        compiler_params=_cp("parallel"),
        name="mla_prep",
    )(c, cos_t, sin_t, gq, gkv, wq, wqr, wk, wv)


def _mla_kernel(q_ref, k_ref, v_ref, o_ref, *, blk):
    i = pl.program_id(2)
    q = q_ref[0]
    qs = (q[:, :128], q[:, 128:])
    first = lax.broadcasted_iota(jnp.int32, (blk, 128), 1) < HEAD
    row = lax.broadcasted_iota(jnp.int32, (blk, blk), 0)
    col = lax.broadcasted_iota(jnp.int32, (blk, blk), 1)
    causal = col <= row

    def step(j, carry, masked):
        m0, l0, m1, l1, acc = carry
        start = pl.multiple_of(j * blk, blk)
        k = k_ref[0, pl.ds(start, blk), :]
        v = v_ref[0, pl.ds(start, blk), :]
        new, outs, alphas = [], [], []
        for h, (m, l) in enumerate(((m0, l0), (m1, l1))):
            z = _dg(qs[h], k[:, h * 128:(h + 1) * 128], NT)
            if masked:
                z = jnp.where(causal, z, -1e30)
            m_new = jnp.maximum(m, jnp.max(z, axis=-1, keepdims=True))
            alpha = jnp.exp(m - m_new)
            p = jnp.exp(z - m_new)
            new += [m_new, alpha * l + jnp.sum(p, axis=-1, keepdims=True)]
            alphas.append(alpha)
            outs.append(_dot(p.astype(BF16), v))
        acc = acc * jnp.where(first, alphas[0], alphas[1]) + jnp.where(first, outs[0], outs[1])
        return new[0], new[1], new[2], new[3], acc

    neg = jnp.full((blk, 1), -1e30, F32)
    zc = jnp.zeros((blk, 1), F32)
    carry = step(i, (neg, zc, neg, zc, jnp.zeros((blk, 128), F32)), True)
    carry = lax.fori_loop(0, i, lambda jj, cr: step(i - 1 - jj, cr, False), carry)
    o_ref[0] = (carry[4] / jnp.where(first, carry[1], carry[3])).astype(o_ref.dtype)


def _mla_attention(q, k, v):
    b, s, _ = q.shape
    blk = ATT_BLOCK
    return pl.pallas_call(
        functools.partial(_mla_kernel, blk=blk),
        grid=(b, 2, s // blk),
        in_specs=[pl.BlockSpec((1, blk, 256), lambda bi, p, i: (bi, i, p)),
                  pl.BlockSpec((1, s, 256), lambda bi, p, i: (bi, 0, p)),
                  pl.BlockSpec((1, s, 128), lambda bi, p, i: (bi, 0, p))],
        out_specs=pl.BlockSpec((1, blk, 128), lambda bi, p, i: (bi, i, p)),
        out_shape=jax.ShapeDtypeStruct((b, s, BRW), BF16),
        compiler_params=_cp("parallel", "parallel", "arbitrary"),
        name="mla_attention",
    )(q, k, v)


def _rwkv_chunk_kernel(p_ref, mu_ref, w0_ref, wwa_ref, a0_ref, gup_ref, kk_ref, ka_ref, rk_ref,
                       q1_ref, q2_ref, m_ref, n_ref, bonus_ref, g_ref, carry_ref, sc_ref,
                       *, tm, tiles_per_seq):
    L = RW_CHUNK
    i = pl.program_id(0)
    p = p_ref[...]

    @pl.when(i % tiles_per_seq == 0)
    def _():
        carry_ref[...] = jnp.zeros_like(carry_ref)

    prev_last = carry_ref[0:1, :]
    rows = lax.broadcasted_iota(jnp.int32, (tm, D), 0)
    prev = jnp.where(rows == 0, prev_last, pltpu.roll(p, 1, axis=0))
    carry_ref[0:1, :] = p[tm - 1:tm, :]
    ps = p + (prev - p) * mu_ref[...]
    r, k, v = ps[:, :256], ps[:, 256:512], ps[:, 512:768]
    wa, g_lo = ps[:, 768:896], ps[:, 896:1024]
    lane = lax.broadcasted_iota(jnp.int32, (tm, 128), 1)
    wa_out = _dot(jnp.where(lane < 64, jnp.tanh(wa), wa), wwa_ref[...], HI)
    w_log = -_softplus(-(w0_ref[...] + wa_out[:, :256])) - 0.5
    lw = -jnp.exp(w_log)
    a = _sigmoid(a0_ref[...] + wa_out[:, 256:])
    g = _dot(_sigmoid(g_lo), gup_ref[...], HI)
    kk = k * kk_ref[...]
    kp = k * (1.0 + (a - 1.0) * ka_ref[...])
    hr = lax.broadcasted_iota(jnp.int32, (BRW, BRW), 0) // HEAD
    hc = lax.broadcasted_iota(jnp.int32, (BRW, BRW), 1) // HEAD
    head_ones = (hr == hc).astype(F32)
    kkn = kk * lax.rsqrt(jnp.maximum(_dot(kk * kk, head_ones, HI), 1e-24))
    bonus = _dot(r * kp * rk_ref[...], head_ones, HI) * v
    beta = -kkn * a
    for qi, arr in enumerate((lw, kkn, beta, kp, r, v, bonus, g)):
        for h in range(4):
            sc_ref[qi, h] = arr[:, h * HEAD:(h + 1) * HEAD]

    tr = lax.broadcasted_iota(jnp.int32, (L, L), 0)
    tc = lax.broadcasted_iota(jnp.int32, (L, L), 1)
    incl = tc <= tr
    strict = tc < tr
    eye = tr == tc
    tri = incl.astype(F32)
    same16 = (tr // 16) == (tc // 16)
    same32 = (tr // 32) == (tc // 32)
    eye_f = eye.astype(F32)

    def unit(u, _):
        c = u // 4
        h = u % 4
        rs = pl.ds(pl.multiple_of(c * L, L), L)
        lw_, kk_, be_, kp_, r_, v_ = (sc_ref[qi, h, rs, :] for qi in range(6))
        cs = _dot(tri, lw_, HI)
        c_last = cs[L - 1:L, :]
        e_neg = jnp.exp(-cs)
        e_end = jnp.exp(c_last - cs)
        kk_s = kk_ * jnp.exp(cs - lw_)
        b_s, k_s, r_s = be_ * e_neg, kp_ * e_neg, r_ * jnp.exp(cs)
        a_ab = jnp.where(strict, _dg(kk_s, b_s, NT, HI), 0.0)
        a_ak = jnp.where(strict, _dg(kk_s, k_s, NT, HI), 0.0)
        a_rb = jnp.where(incl, _dg(r_s, b_s, NT, HI), 0.0)
        a_rk = jnp.where(incl, _dg(r_s, k_s, NT, HI), 0.0)
        x = jnp.where(same16, a_ab, 0.0)
        ti = eye_f + x
        for _ in range(3):
            x = _dot(x, x, HI)
            ti = ti + _dot(ti, x, HI)
        off32 = jnp.where(same32 & jnp.logical_not(same16), a_ab, 0.0)
        ti = ti + _dot(ti, _dot(off32, ti, HI), HI)
        off64 = jnp.where(same32, 0.0, a_ab)
        ti = ti + _dot(ti, _dot(off64, ti, HI), HI)
        p1 = _dot(ti, kk_s, HI)
        p2 = _dot(ti, _dot(a_ak, v_, HI), HI)
        q1_ref[c, h] = r_s + _dot(a_rb, p1, HI)
        q2_ref[c, h] = _dot(a_rb, p2, HI) + _dot(a_rk, v_, HI)
        bg, kg = be_ * e_end, kp_ * e_end
        m_ref[c, h] = jnp.where(eye, jnp.exp(c_last), 0.0) + _dg(p1, bg, TN, HI)
        n_ref[c, h] = _dg(p2, bg, TN, HI) + _dg(v_, kg, TN, HI)
        bonus_ref[c, h] = sc_ref[6, h, rs, :]
        g_ref[c, h] = sc_ref[7, h, rs, :]
        return 0

    lax.fori_loop(0, (tm // L) * 4, unit, 0)


def _rwkv_chunks(p, seq, mu, w0, wwa, a0, gup, k_k, k_a, r_k, tm):
    t = p.shape[0]
    nc = tm // RW_CHUNK
    out_sd = jax.ShapeDtypeStruct((t // RW_CHUNK, 4, RW_CHUNK, HEAD), F32)
    out_spec = pl.BlockSpec((nc, 4, RW_CHUNK, HEAD), lambda i: (i, 0, 0, 0))
    return pl.pallas_call(
        functools.partial(_rwkv_chunk_kernel, tm=tm, tiles_per_seq=seq // tm),
        grid=(t // tm,),
        in_specs=[pl.BlockSpec((tm, D), lambda i: (i, 0)), _full((1, D)), _full((1, BRW)),
                  _full((128, 512)), _full((1, BRW)), _full((128, BRW)), _full((1, BRW)),
                  _full((1, BRW)), _full((1, BRW))],
        out_specs=[out_spec] * 6,
        out_shape=[out_sd] * 6,
        scratch_shapes=[pltpu.VMEM((8, D), F32), pltpu.VMEM((8, 4, tm, HEAD), F32)],
        compiler_params=_cp("arbitrary"),
        name="rwkv_chunks",
    )(p, mu, w0, wwa, a0, gup, k_k, k_a, r_k)


def _rwkv_scan_kernel(q1_ref, q2_ref, m_ref, n_ref, bonus_ref, g_ref, gng_ref, gnb_ref, o_ref, state_ref, *, nc):
    state_ref[...] = jnp.zeros_like(state_ref)

    def body(c, _):
        for h in range(4):
            s = state_ref[h]
            y = _dg(q1_ref[c, h], s, NT, HI) + q2_ref[c, h]
            mu = jnp.mean(y, axis=-1, keepdims=True)
            d = y - mu
            var = jnp.mean(d * d, axis=-1, keepdims=True)
            yn = d * lax.rsqrt(var + RW_GN_EPS) * gng_ref[h] + gnb_ref[h]
            o_ref[c, h] = ((yn + bonus_ref[c, h]) * g_ref[c, h]).astype(o_ref.dtype)
            state_ref[h] = _dot(s, m_ref[c, h], HI) + n_ref[c, h]
        return 0

    lax.fori_loop(0, nc, body, 0)


def _rwkv_scan(q1, q2, m, n, bonus, g, gn_g, gn_b, batch):
    nct = q1.shape[0]
    nc = nct // batch
    spec = pl.BlockSpec((nc, 4, RW_CHUNK, HEAD), lambda b: (b, 0, 0, 0))
    return pl.pallas_call(
        functools.partial(_rwkv_scan_kernel, nc=nc),
        grid=(batch,),
        in_specs=[spec] * 6 + [_full((4, 1, HEAD)), _full((4, 1, HEAD))],
        out_specs=spec,
        out_shape=jax.ShapeDtypeStruct((nct, 4, RW_CHUNK, HEAD), BF16),
        scratch_shapes=[pltpu.VMEM((4, HEAD, HEAD), F32)],
        compiler_params=_cp("parallel"),
        name="rwkv_scan",
    )(q1, q2, m, n, bonus, g, gn_g, gn_b)


def _ssd_kernel(xp_ref, z_ref, dtc_ref, dtr_ref, cw_ref, cb_ref, dtb_row_ref, dtb_col_ref,
                alog_row_ref, alog_col_ref, dskip_ref, ng_ref, o_ref, state_ref):
    L = SSD_CHUNK
    c = pl.program_id(1)

    @pl.when(c == 0)
    def _():
        state_ref[...] = jnp.zeros_like(state_ref)

    win = xp_ref[0, pl.ds(pl.multiple_of(c * L, L), L + 8), :]
    conv = cb_ref[...] + sum(cw_ref[kk:kk + 1, :] * win[5 + kk:5 + kk + L, :] for kk in range(4))
    u = conv * _sigmoid(conv)
    xs, bm, cm = u[:, :256], u[:, 256:512], u[:, 512:768]

    dt_c = _softplus(dtc_ref[0] + dtb_row_ref[...])
    adt_c = dt_c * (-jnp.exp(alog_row_ref[...]))
    er = lax.broadcasted_iota(jnp.int32, (128, 256), 0)
    ec = lax.broadcasted_iota(jnp.int32, (128, 256), 1)
    expand_head = (er == ec // HEAD).astype(F32)
    er5 = lax.broadcasted_iota(jnp.int32, (128, 512), 0)
    ec5 = lax.broadcasted_iota(jnp.int32, (128, 512), 1)
    expand_full = (er5 == ec5 // 128).astype(F32)
    tr = lax.broadcasted_iota(jnp.int32, (L, L), 0)
    tc = lax.broadcasted_iota(jnp.int32, (L, L), 1)
    lower = tc <= tr
    tri = lower.astype(F32)
    dt_full = _dot(dt_c, expand_head, HI)
    acs_w = _dot(tri, _dot(adt_c, expand_full, HI), HI)
    dt_r = _softplus(dtr_ref[0] + dtb_col_ref[...])
    adt_r = dt_r * (-jnp.exp(alog_col_ref[...]))
    acs_r = _dg(adt_r, tri, NT, HI)

    xd = (xs * dt_full).astype(BF16)
    bb, cbf = bm.astype(BF16), cm.astype(BF16)
    first = lax.broadcasted_iota(jnp.int32, (L, 128), 1) < HEAD
    for p in range(2):
        sl = slice(p * 128, (p + 1) * 128)
        bg, cg, xdp = bb[:, sl], cbf[:, sl], xd[:, sl]
        cb = _dg(cg, bg, NT)
        yd, st = [], []
        for hh in range(2):
            h = 2 * p + hh
            colv = acs_w[:, h * 128:(h + 1) * 128]
            seg = colv - acs_r[h:h + 1, :]
            dec = jnp.where(lower, jnp.exp(jnp.minimum(seg, 0.0)), 0.0)
            yd.append(_dot((cb * dec).astype(BF16), xdp))
            dte = jnp.exp(colv[L - 1:L, :] - colv)
            st.append(_dg((bm[:, sl] * dte).astype(BF16), xdp, TN))
        acs_pair = jnp.where(first, acs_w[:, (2 * p) * 128:(2 * p + 1) * 128],
                             acs_w[:, (2 * p + 1) * 128:(2 * p + 2) * 128])
        s_in = state_ref[p]
        y = jnp.where(first, yd[0], yd[1]) + jnp.exp(acs_pair) * _dot(cg, s_in.astype(BF16))
        state_ref[p] = jnp.exp(acs_pair[L - 1:L, :]) * s_in + jnp.where(first, st[0], st[1])
        y = y + dskip_ref[:, sl] * xs[:, sl]
        zz = z_ref[0][:, sl]
        y = y * (zz * _sigmoid(zz))
        yn = y * lax.rsqrt(jnp.mean(y * y, axis=-1, keepdims=True) + 1e-5) * ng_ref[:, sl]
        o_ref[0, :, sl] = yn.astype(o_ref.dtype)


def _ssd(xp, z, dtc, dtr, cw, cb, dtb_row, dtb_col, alog_row, alog_col, dskip, ng):
    b, sp, _ = xp.shape
    s = sp - 8
    L = SSD_CHUNK
    return pl.pallas_call(
        _ssd_kernel,
        grid=(b, s // L),
        in_specs=[pl.BlockSpec((1, sp, 768), lambda bi, c: (bi, 0, 0)),
                  pl.BlockSpec((1, L, 256), lambda bi, c: (bi, c, 0)),
                  pl.BlockSpec((1, L, 128), lambda bi, c: (bi, c, 0)),
                  pl.BlockSpec((1, 8, L), lambda bi, c: (bi, 0, c)),
                  _full((8, 768)), _full((1, 768)), _full((1, 128)), _full((8, 128)),
                  _full((1, 128)), _full((8, 128)), _full((1, 256)), _full((1, 256))],
        out_specs=pl.BlockSpec((1, L, 256), lambda bi, c: (bi, c, 0)),
        out_shape=jax.ShapeDtypeStruct((b, s, BRW), BF16),
        scratch_shapes=[pltpu.VMEM((2, SSD_N, 128), F32)],
        compiler_params=_cp("parallel", "arbitrary"),
        name="ssd",
    )(xp, z, dtc, dtr, cw, cb, dtb_row, dtb_col, alog_row, alog_col, dskip, ng)


def _mixout_kernel(x_ref, y0_ref, y1_ref, y2_ref, y3_ref, wg_ref, wbr_ref, wo_ref, g_ref, b_ref, o_ref):
    x = x_ref[...]
    xb = x.astype(BF16)
    merged = None
    for i, y_ref in enumerate((y0_ref, y1_ref, y2_ref, y3_ref)):
        gate = _sigmoid(_dot(xb, wg_ref[:, i * D:(i + 1) * D]))
        term = gate * _dot(y_ref[...], wbr_ref[i])
        merged = term if merged is None else merged + term
    mix = _dot(merged.astype(BF16), wo_ref[...])
    o_ref[...] = _layer_norm(DN_ALPHA * x + mix, g_ref[...], b_ref[...])


def _mixout(x, ys, wg, wbr, wo, g, b, tm):
    t = x.shape[0]
    row = lambda n: pl.BlockSpec((tm, n), lambda i: (i, 0))
    return pl.pallas_call(
        _mixout_kernel,
        grid=(t // tm,),
        in_specs=[row(D)] + [row(BRW)] * 4 + [_full((D, 4 * D)), _full((4, BRW, D)), _full((D, D)),
                                               _full((1, D)), _full((1, D))],
        out_specs=row(D),
        out_shape=jax.ShapeDtypeStruct((t, D), F32),
        compiler_params=_cp("parallel"),
        name="mixout",
    )(x, *ys, wg, wbr, wo, g, b)


def _xattn_kernel(x_ref, k_ref, v_ref, wq_ref, wo_ref, g_ref, b_ref, o_ref):
    x = x_ref[0]
    q = (_dot(x.astype(BF16), wq_ref[...]) * (XA_DH ** -0.5)).astype(BF16)
    k, v = k_ref[0], v_ref[0]
    outs = []
    for h in range(XA_HEADS):
        sl = slice(h * XA_DH, (h + 1) * XA_DH)
        z = _dg(q[:, sl], k[:, sl], NT)
        e = jnp.exp(z - jnp.max(z, axis=-1, keepdims=True))
        prob = e / jnp.sum(e, axis=-1, keepdims=True)
        outs.append(_dot(prob.astype(BF16), v[:, sl]).astype(BF16))
    xa = _dot(jnp.concatenate(outs, axis=1), wo_ref[...])
    o_ref[0] = _layer_norm(DN_ALPHA * x + xa, g_ref[...], b_ref[...])


def _xattn(x, k, v, wq, wo, g, b, tm):
    bsz, s, _ = x.shape
    return pl.pallas_call(
        _xattn_kernel,
        grid=(bsz, s // tm),
        in_specs=[pl.BlockSpec((1, tm, D), lambda bi, i: (bi, i, 0)),
                  pl.BlockSpec((1, MEM, D), lambda bi, i: (bi, 0, 0)),
                  pl.BlockSpec((1, MEM, D), lambda bi, i: (bi, 0, 0)),
                  _full((D, D)), _full((D, D)), _full((1, D)), _full((1, D))],
        out_specs=pl.BlockSpec((1, tm, D), lambda bi, i: (bi, i, 0)),
        out_shape=jax.ShapeDtypeStruct((bsz, s, D), F32),
        compiler_params=_cp("parallel", "parallel"),
        name="xattn",
    )(x, k, v, wq, wo, g, b)


def _ffn_kernel(x_ref, w13_ref, w2_ref, g_ref, b_ref, o_ref, acc_ref, *, hidden, fchunk):
    x = x_ref[...]
    xb = x.astype(BF16)
    for c in range(hidden // fchunk):
        gate = _dot(xb, w13_ref[:, c * fchunk:(c + 1) * fchunk])
        up = _dot(xb, w13_ref[:, hidden + c * fchunk:hidden + (c + 1) * fchunk])
        hcol = (gate * _sigmoid(gate) * up).astype(BF16)
        contrib = _dot(hcol, w2_ref[c * fchunk:(c + 1) * fchunk, :])
        if c == 0:
            acc_ref[...] = contrib
        else:
            acc_ref[...] += contrib
    o_ref[...] = _layer_norm(DN_ALPHA * x + acc_ref[...], g_ref[...], b_ref[...])


def _ffn(x, w13, w2, g, b, tm):
    t = x.shape[0]
    hidden = w2.shape[0]
    row = pl.BlockSpec((tm, D), lambda i: (i, 0))
    return pl.pallas_call(
        functools.partial(_ffn_kernel, hidden=hidden, fchunk=FFN_FCHUNK),
        grid=(t // tm,),
        in_specs=[row, _full((D, 2 * hidden)), _full((hidden, D)), _full((1, D)), _full((1, D))],
        out_specs=row,
        out_shape=jax.ShapeDtypeStruct((t, D), F32),
        scratch_shapes=[pltpu.VMEM((tm, D), F32)],
        compiler_params=_cp("parallel"),
        name="ffn",
    )(x, w13, w2, g, b)


def _router_kernel(x_ref, wr_ref, meta_ref, counts_ref, run_ref, *, tm):
    i = pl.program_id(0)

    @pl.when(i == 0)
    def _():
        run_ref[...] = jnp.zeros_like(run_ref)

    logits = _dot(x_ref[...], wr_ref[...], HI)
    lane = lax.broadcasted_iota(jnp.int32, (tm, 128), 1)
    logits = jnp.where(lane < N_EXPERTS, logits, -jnp.inf)
    v1 = jnp.max(logits, axis=-1, keepdims=True)
    i1 = jnp.min(jnp.where(logits == v1, lane, 128), axis=-1, keepdims=True)
    rest = jnp.where(lane == i1, -jnp.inf, logits)
    v2 = jnp.max(rest, axis=-1, keepdims=True)
    i2 = jnp.min(jnp.where(rest == v2, lane, 128), axis=-1, keepdims=True)
    e2 = jnp.exp(v2 - v1)
    g1 = 1.0 / (1.0 + e2)
    g2 = e2 / (1.0 + e2)
    member = ((lane == i1) | (lane == i2))
    tr = lax.broadcasted_iota(jnp.int32, (tm, tm), 0)
    tc = lax.broadcasted_iota(jnp.int32, (tm, tm), 1)
    before = (tc < tr).astype(BF16)
    memb = jnp.where(member, 1.0, 0.0).astype(BF16)
    rank = run_ref[0:1, :] + _dot(before, memb)
    r1 = jnp.sum(jnp.where(lane == i1, rank, 0.0), axis=-1, keepdims=True)
    r2 = jnp.sum(jnp.where(lane == i2, rank, 0.0), axis=-1, keepdims=True)
    run_ref[0:1, :] = run_ref[0:1, :] + jnp.sum(memb.astype(F32), axis=0, keepdims=True)
    counts_ref[...] = jnp.broadcast_to(run_ref[0:1, :], counts_ref.shape)
    meta = jnp.where(lane == 0, i1.astype(F32), 0.0)
    meta = jnp.where(lane == 1, i2.astype(F32), meta)
    meta = jnp.where(lane == 2, r1, meta)
    meta = jnp.where(lane == 3, r2, meta)
    meta = jnp.where(lane == 4, g1, meta)
    meta = jnp.where(lane == 5, g2, meta)
    meta_ref[...] = meta


def _router(x, wr, tm):
    t = x.shape[0]
    return pl.pallas_call(
        functools.partial(_router_kernel, tm=tm),
        grid=(t // tm,),
        in_specs=[pl.BlockSpec((tm, D), lambda i: (i, 0)), _full((D, 128))],
        out_specs=[pl.BlockSpec((tm, 128), lambda i: (i, 0)), _full((8, 128))],
        out_shape=[jax.ShapeDtypeStruct((t, 128), F32), jax.ShapeDtypeStruct((8, 128), F32)],
        scratch_shapes=[pltpu.VMEM((8, 128), F32)],
        compiler_params=_cp("arbitrary"),
        name="router",
    )(x, wr)


def _gather_kernel(idx_ref, src_ref, o_ref, sem, *, tm):
    base = pl.program_id(0) * tm

    def issue(r, _):
        pltpu.make_async_copy(src_ref.at[pl.ds(idx_ref[base + r], 1)], o_ref.at[pl.ds(r, 1)], sem).start()
        return 0

    lax.fori_loop(0, tm, issue, 0)

    def drain(r, _):
        pltpu.make_async_copy(src_ref.at[pl.ds(0, 1)], o_ref.at[pl.ds(r, 1)], sem).wait()
        return 0

    lax.fori_loop(0, tm, drain, 0)


def _gather_rows(src, idx, tm):
    n = idx.shape[0]
    width = src.shape[1]
    return pl.pallas_call(
        functools.partial(_gather_kernel, tm=tm),
        grid_spec=pltpu.PrefetchScalarGridSpec(
            num_scalar_prefetch=1,
            grid=(n // tm,),
            in_specs=[pl.BlockSpec(memory_space=pl.ANY)],
            out_specs=pl.BlockSpec((tm, width), lambda i, idx_ref: (i, 0)),
            scratch_shapes=[pltpu.SemaphoreType.DMA(())],
        ),
        out_shape=jax.ShapeDtypeStruct((n, width), src.dtype),
        compiler_params=_cp("arbitrary"),
        name="gather_rows",
    )(idx, src)


def _moe_kernel(be_ref, nb_ref, x_ref, wg_ref, wu_ref, w2_ref, o_ref, acc_ref):
    i, j = pl.program_id(0), pl.program_id(1)

    @pl.when(i < nb_ref[0])
    def _():
        xb = x_ref[...].astype(BF16)
        gate = _dot(xb, wg_ref[0])
        up = _dot(xb, wu_ref[0])
        contrib = _dot((gate * _sigmoid(gate) * up).astype(BF16), w2_ref[0])

        @pl.when(j == 0)
        def _():
            acc_ref[...] = contrib

        @pl.when(j > 0)
        def _():
            acc_ref[...] += contrib

    @pl.when(j == pl.num_programs(1) - 1)
    def _():
        o_ref[...] = jnp.where(i < nb_ref[0], acc_ref[...], 0.0)


def _moe_blocks(xs, block_e, n_used, w13, w2):
    n_rows = xs.shape[0]
    nf = FFN_EXPERT // MOE_FCHUNK
    return pl.pallas_call(
        _moe_kernel,
        grid_spec=pltpu.PrefetchScalarGridSpec(
            num_scalar_prefetch=2,
            grid=(n_rows // MOE_ROWS, nf),
            in_specs=[pl.BlockSpec((MOE_ROWS, D), lambda i, j, be, nb: (i, 0)),
                      pl.BlockSpec((1, D, MOE_FCHUNK), lambda i, j, be, nb: (be[i], 0, j)),
                      pl.BlockSpec((1, D, MOE_FCHUNK), lambda i, j, be, nb: (be[i], 0, nf + j)),
                      pl.BlockSpec((1, MOE_FCHUNK, D), lambda i, j, be, nb: (be[i], j, 0))],
            out_specs=pl.BlockSpec((MOE_ROWS, D), lambda i, j, be, nb: (i, 0)),
            scratch_shapes=[pltpu.VMEM((MOE_ROWS, D), F32)],
        ),
        out_shape=jax.ShapeDtypeStruct((n_rows, D), F32),
        compiler_params=_cp("parallel", "arbitrary"),
        name="moe_blocks",
    )(block_e, n_used, xs, w13, w13, w2)


def _combine_kernel(x_ref, ya_ref, yb_ref, meta_ref, g_ref, b_ref, o_ref):
    meta = meta_ref[...]
    ffn = meta[:, 4:5] * ya_ref[...] + meta[:, 5:6] * yb_ref[...]
    o_ref[...] = _layer_norm(DN_ALPHA * x_ref[...] + ffn, g_ref[...], b_ref[...])


def _combine(x, ya, yb, meta, g, b, tm):
    t = x.shape[0]
    row = lambda n: pl.BlockSpec((tm, n), lambda i: (i, 0))
    return pl.pallas_call(
        _combine_kernel,
        grid=(t // tm,),
        in_specs=[row(D), row(D), row(D), row(128), _full((1, D)), _full((1, D))],
        out_specs=row(D),
        out_shape=jax.ShapeDtypeStruct((t, D), F32),
        compiler_params=_cp("parallel"),
        name="moe_combine",
    )(x, ya, yb, meta, g, b)


def _moe_layer(x2, router, w13, w2, g, b, tm):
    t = x2.shape[0]
    wr = jnp.zeros((D, 128), F32).at[:, :N_EXPERTS].set(router)
    meta, counts = _router(x2, wr, tm)
    cnt = counts[0, :N_EXPERTS].astype(jnp.int32)
    padded = (cnt + MOE_ROWS - 1) // MOE_ROWS * MOE_ROWS
    p_ends = jnp.cumsum(padded)
    p_starts = p_ends - padded
    e1, e2 = meta[:, 0].astype(jnp.int32), meta[:, 1].astype(jnp.int32)
    d1 = p_starts[e1] + meta[:, 2].astype(jnp.int32)
    d2 = p_starts[e2] + meta[:, 3].astype(jnp.int32)
    n_blocks = (2 * t) // MOE_ROWS + N_EXPERTS
    n_rows = n_blocks * MOE_ROWS
    tok = jnp.arange(t, dtype=jnp.int32)
    row_tok = jnp.zeros((n_rows,), jnp.int32).at[d1].set(tok).at[d2].set(tok)
    blk_start = jnp.arange(n_blocks, dtype=jnp.int32) * MOE_ROWS
    block_e = jnp.minimum(jnp.searchsorted(p_ends, blk_start, side="right"), N_EXPERTS - 1).astype(jnp.int32)
    n_used = (p_ends[-1:] // MOE_ROWS).astype(jnp.int32)
    xs = _gather_rows(x2, row_tok, MOE_ROWS)
    ys = _moe_blocks(xs, block_e, n_used, w13, w2)
    ya = _gather_rows(ys, d1, tm)
    yb = _gather_rows(ys, d2, tm)
    return _combine(x2, ya, yb, meta, g, b, tm)


def _rot_half_cols(w):
    h = MLA_ROPE // 2
    return jnp.concatenate([-w[:, h:], w[:, :h]], axis=1)


def _arrange_in_proj(w_in):
    sb, rw, ssd, mla = jnp.split(w_in, [768, 768 + 1024, 768 + 1024 + 1028], axis=1)
    sb = sb.at[:, :256].multiply(HEAD ** -0.5)
    z, xbc, dt = ssd[:, :256], ssd[:, 256:1024], ssd[:, 1024:]
    cq, ckv, kr = mla[:, :256], mla[:, 256:384], mla[:, 384:]
    pad = lambda w, lo, n: jnp.zeros((D, n), F32).at[:, lo:lo + w.shape[1]].set(w)
    cols = [sb, rw, z, xbc, pad(dt, 0, 128), cq, ckv, pad(kr, 64, 128), pad(_rot_half_cols(kr), 64, 128)]
    return jnp.concatenate(cols, axis=1).astype(BF16)


def _arrange_mla(w_uq, w_ukv):
    wq = jnp.zeros((MLA_Q_RANK, 512), F32)
    wqr = jnp.zeros((MLA_Q_RANK, 512), F32)
    wk = jnp.zeros((MLA_KV_RANK, 512), F32)
    wv = []
    for h in range(MLA_HEADS):
        qh = w_uq[:, h * 96:(h + 1) * 96]
        wq = wq.at[:, h * 128:h * 128 + 96].set(qh)
        wqr = wqr.at[:, h * 128 + 64:h * 128 + 96].set(_rot_half_cols(qh[:, 64:]))
        wk = wk.at[:, h * 128:h * 128 + 64].set(w_ukv[:, h * 128:h * 128 + 64])
        wv.append(w_ukv[:, h * 128 + 64:(h + 1) * 128])
    return wq.astype(BF16), wqr.astype(BF16), wk.astype(BF16), jnp.concatenate(wv, axis=1).astype(BF16)


def kernel(x, mem, positions, mix_w_in, rw_mu, rw_w0, rw_w_up, rw_a0, rw_a_up, rw_g_up, rw_k_k, rw_k_a, rw_r_k, rw_gn_g, rw_gn_b, ssd_conv_w, ssd_conv_b, ssd_dt_bias, ssd_a_log, ssd_d, ssd_norm_g, mla_q_norm_g, mla_w_uq, mla_kv_norm_g, mla_w_ukv, mix_w_br_out, mix_w_gate, mix_w_out, ln1_g, ln1_b, xa_w_q, xa_w_kv, xa_w_o, ln2_g, ln2_b, ffn_w13, ffn_w2, moe_router, moe_w13, moe_w2, ln3_g, ln3_b):
    b, s, _ = x.shape
    t = b * s
    tm = min(ROW_TILE, s)
    rw_tm = min(256, s)

    inv_freq = ROPE_BASE ** (-jnp.arange(0, MLA_ROPE, 2, dtype=F32) / MLA_ROPE)
    ang = positions.astype(F32).reshape(t, 1) * inv_freq
    cos, sin = jnp.cos(ang), jnp.sin(ang)
    cos_t = jnp.concatenate([jnp.ones((t, 64), F32), cos, cos, jnp.zeros((t, 32), F32)], axis=1)
    sin_t = jnp.concatenate([jnp.zeros((t, 64), F32), sin, sin, jnp.zeros((t, 32), F32)], axis=1)

    xf = x.reshape(t, D)
    memf = mem.reshape(b * MEM, D)
    row2 = lambda a: a.reshape(1, -1)
    for l in range(DEPTH):
        w_in = _arrange_in_proj(mix_w_in[l])
        sb_q, sb_k, sb_v, rw_p, ssd_z, ssd_xbc, ssd_dt, mla_c = _rowmm(
            xf, w_in, [(256, BF16), (256, BF16), (256, BF16), (1024, F32), (256, F32), (768, F32),
                       (128, F32), (640, F32)], tm)

        y_sb = _sb_attention(sb_q.reshape(b, s, BRW), sb_k.reshape(b, s, BRW), sb_v.reshape(b, s, BRW))

        wwa = jnp.zeros((128, 512), F32).at[:64, :256].set(rw_w_up[l]).at[64:, 256:].set(rw_a_up[l])
        q1, q2, mm, nn, bonus, gg = _rwkv_chunks(
            rw_p, s, row2(rw_mu[l]), row2(rw_w0[l]), wwa, row2(rw_a0[l]), rw_g_up[l], row2(rw_k_k[l]),
            row2(rw_k_a[l]), row2(rw_r_k[l]), rw_tm)
        y_rw = _rwkv_scan(q1, q2, mm, nn, bonus, gg, rw_gn_g[l].reshape(4, 1, HEAD),
                          rw_gn_b[l].reshape(4, 1, HEAD), b)
        y_rw = y_rw.reshape(t // RW_CHUNK, 4, RW_CHUNK, HEAD).transpose(0, 2, 1, 3).reshape(t, BRW)

        xp = jnp.pad(ssd_xbc.reshape(b, s, 768), ((0, 0), (8, 0), (0, 0)))
        dtr = jnp.pad(ssd_dt.reshape(b, s, 128)[:, :, :4].transpose(0, 2, 1), ((0, 0), (0, 4), (0, 0)))
        pad128 = lambda a: jnp.zeros((1, 128), F32).at[0, :4].set(a)
        col8 = lambda a: jnp.zeros((8, 128), F32).at[:4, :].set(jnp.broadcast_to(a[:, None], (4, 128)))
        cw = jnp.zeros((8, 768), F32).at[:4].set(ssd_conv_w[l])
        y_ssd = _ssd(xp, ssd_z.reshape(b, s, 256), ssd_dt.reshape(b, s, 128), dtr, cw, row2(ssd_conv_b[l]),
                     pad128(ssd_dt_bias[l]), col8(ssd_dt_bias[l]), pad128(ssd_a_log[l]), col8(ssd_a_log[l]),
                     row2(jnp.repeat(ssd_d[l], HEAD)), row2(ssd_norm_g[l]))

        wq, wqr, wk, wv = _arrange_mla(mla_w_uq[l], mla_w_ukv[l])
        mq, mk, mv = _mla_prep(mla_c, cos_t, sin_t, row2(mla_q_norm_g[l]), row2(mla_kv_norm_g[l]),
                               wq, wqr, wk, wv, tm)
        y_mla = _mla_attention(mq.reshape(b, s, 512), mk.reshape(b, s, 512), mv.reshape(b, s, 256))

        wg = jnp.concatenate([mix_w_gate[l, i] for i in range(4)], axis=1).astype(BF16)
        x1 = _mixout(xf, (y_sb.reshape(t, BRW), y_rw, y_ssd.reshape(t, BRW), y_mla.reshape(t, BRW)),
                     wg, mix_w_br_out[l].astype(BF16), mix_w_out[l].astype(BF16),
                     row2(ln1_g[l]), row2(ln1_b[l]), min(256, s))

        xk, xv = _rowmm(memf, xa_w_kv[l].astype(BF16), [(D, BF16), (D, BF16)], MEM)
        x2 = _xattn(x1.reshape(b, s, D), xk.reshape(b, MEM, D), xv.reshape(b, MEM, D),
                    xa_w_q[l].astype(BF16), xa_w_o[l].astype(BF16), row2(ln2_g[l]), row2(ln2_b[l]),
                    min(256, s)).reshape(t, D)

        if l % 2 == 0:
            xf = _ffn(x2, ffn_w13[l // 2].astype(BF16), ffn_w2[l // 2].astype(BF16),
                      row2(ln3_g[l]), row2(ln3_b[l]), tm)
        else:
            xf = _moe_layer(x2, moe_router[l // 2], moe_w13[l // 2].astype(BF16),
                            moe_w2[l // 2].astype(BF16), row2(ln3_g[l]), row2(ln3_b[l]), tm)
    return xf.reshape(b, s, D)
```
